```python
import math
import jax, jax.numpy as jnp
from jax import lax
import numpy as np

D_MODEL = 2048
BATCH = 4
SEQ = 4096
DEPTH = 2

N_MIXERS = 2
CHUNK = 128
D_FFN_A = 4 * D_MODEL
D_HALF_A = D_FFN_A // 2
N_GROUPS_A = 16
D_GROUP_A = D_HALF_A // N_GROUPS_A
POOL_WINDOWS = (2, 4, 8, 16)
N_GROUPS_B = len(POOL_WINDOWS)
D_MIX_B = D_MODEL
D_GROUP_B = D_MIX_B // N_GROUPS_B
N_KEYS = 128
N_EXPERTS = N_KEYS * N_KEYS
PEER_HEADS = 8
D_KEY = 128
D_QUERY = 2 * D_KEY
TOPK_HALF = 16
TOPK = 16
PEER_BLOCK = 64
N_A_LAYERS = (DEPTH + 1) // 2
N_B_LAYERS = DEPTH // 2
ALPHA = (2 * DEPTH) ** 0.25
BETA = (8 * DEPTH) ** -0.25
LN_EPS = 1e-5

kernel_name = 'hybrid_gmlp_pool_peer_deepnorm'


def layer_norm(x, g, b):
    xf = x.astype(jnp.float32)
    mu = jnp.mean(xf, axis=-1, keepdims=True)
    xc = xf - mu
    var = jnp.mean(xc * xc, axis=-1, keepdims=True)
    y = xc * lax.rsqrt(var + LN_EPS) * g.astype(jnp.float32) + b.astype(jnp.float32)
    return y.astype(x.dtype)


def mixer_gmlp_chunked(x, w_in, b_in, v_g, v_b, w_s, b_s, w_out):
    B, S, _ = x.shape
    z = jax.nn.gelu(x @ w_in + b_in)
    u, v = z[..., :D_HALF_A], z[..., D_HALF_A:]
    v = layer_norm(v, v_g, v_b)
    v = v.reshape(B, S // CHUNK, CHUNK, N_GROUPS_A, D_GROUP_A)
    causal = jnp.tril(jnp.ones((CHUNK, CHUNK), dtype=bool))
    ws = jnp.where(causal[None], w_s, 0.0)
    sv = jnp.einsum('hts,bcshd->bcthd', ws, v) + b_s.T[None, None, :, :, None]
    gated = u * sv.reshape(B, S, D_HALF_A)
    return gated @ w_out


def mixer_pool_multiscale(x, w_in, w_grp, b_grp, scale, w_out):
    B, S, _ = x.shape
    h = (x @ w_in).reshape(B, S, N_GROUPS_B, D_GROUP_B)
    c = lax.cumsum(h.astype(jnp.float32), axis=1)
    pos = jnp.arange(S)
    pooled = []
    for g, w in enumerate(POOL_WINDOWS):
        cg = c[:, :, g]
        shifted = jnp.pad(cg, ((0, 0), (w, 0), (0, 0)))[:, :S]
        count = jnp.minimum(pos + 1, w).astype(jnp.float32)[None, :, None]
        pooled.append((cg - shifted) / count)
    pooled = jnp.stack(pooled, axis=2).astype(x.dtype) - h
    mixed = jnp.einsum('bsgc,gcd->bsgd', pooled, w_grp) + b_grp
    return (mixed.reshape(B, S, D_MIX_B) * scale) @ w_out


def peer_ffn(x, wq, subkeys, u_tab, v_tab):
    B, S, D = x.shape
    nb = S // PEER_BLOCK
    xb = x.reshape(B, nb, PEER_BLOCK, D).transpose(1, 0, 2, 3)

    def block(xc):
        q = (xc @ wq).reshape(B, PEER_BLOCK, PEER_HEADS, 2, D_KEY)
        s = jnp.einsum('bthpk,pnk->bthpn', q, subkeys).astype(jnp.float32)
        s1, i1 = lax.top_k(s[..., 0, :], TOPK_HALF)
        s2, i2 = lax.top_k(s[..., 1, :], TOPK_HALF)
        n_cand = TOPK_HALF * TOPK_HALF
        cand_s = (s1[..., :, None] + s2[..., None, :]).reshape(B, PEER_BLOCK, PEER_HEADS, n_cand)
        cand_i = (i1[..., :, None] * N_KEYS + i2[..., None, :]).reshape(B, PEER_BLOCK, PEER_HEADS, n_cand)
        top_s, sel = lax.top_k(cand_s, TOPK)
        eidx = jnp.take_along_axis(cand_i, sel, axis=-1)
        gate = jax.nn.softmax(top_s, axis=-1)
        u = u_tab[eidx]
        pre = jnp.einsum('btd,bthkd->bthk', xc, u)
        act = (gate * jax.nn.gelu(pre.astype(jnp.float32))).astype(xc.dtype)
        v = v_tab[eidx]
        return jnp.einsum('bthk,bthkd->btd', act, v)

    y = lax.map(block, xb)
    return y.transpose(1, 0, 2, 3).reshape(B, S, D)


def setup_inputs(seed: int = 0) -> dict:
    key = jax.random.key(seed)
    ks = jax.random.split(key, 24)
    f32 = jnp.float32
    nrm = lambda k, shp: jax.random.normal(k, shp, dtype=f32)
    x = nrm(ks[0], (BATCH, SEQ, D_MODEL))
    a_w_in = nrm(ks[1], (N_A_LAYERS, D_MODEL, D_FFN_A)) * D_MODEL ** -0.5
    a_b_in = 0.02 * nrm(ks[2], (N_A_LAYERS, D_FFN_A))
    a_v_g = 1.0 + 0.02 * nrm(ks[3], (N_A_LAYERS, D_HALF_A))
    a_v_b = 0.02 * nrm(ks[4], (N_A_LAYERS, D_HALF_A))
    a_w_s = nrm(ks[5], (N_A_LAYERS, N_GROUPS_A, CHUNK, CHUNK)) * CHUNK ** -0.5
    a_b_s = 1.0 + 0.02 * nrm(ks[6], (N_A_LAYERS, N_GROUPS_A, CHUNK))
    a_w_out = nrm(ks[7], (N_A_LAYERS, D_HALF_A, D_MODEL)) * (D_HALF_A ** -0.5 * BETA)
    b_w_in = nrm(ks[8], (N_B_LAYERS, D_MODEL, D_MIX_B)) * D_MODEL ** -0.5
    b_w_grp = nrm(ks[9], (N_B_LAYERS, N_GROUPS_B, D_GROUP_B, D_GROUP_B)) * D_GROUP_B ** -0.5
    b_b_grp = 0.02 * nrm(ks[10], (N_B_LAYERS, N_GROUPS_B, D_GROUP_B))
    b_scale = 1.0 + 0.02 * nrm(ks[11], (N_B_LAYERS, D_MIX_B))
    b_w_out = nrm(ks[12], (N_B_LAYERS, D_MIX_B, D_MODEL)) * (D_MIX_B ** -0.5 * BETA)
    ln_mix_g = 1.0 + 0.02 * nrm(ks[13], (DEPTH, D_MODEL))
    ln_mix_b = 0.02 * nrm(ks[14], (DEPTH, D_MODEL))
    ln_ffn_g = 1.0 + 0.02 * nrm(ks[15], (DEPTH, D_MODEL))
    ln_ffn_b = 0.02 * nrm(ks[16], (DEPTH, D_MODEL))
    peer_wq = nrm(ks[17], (DEPTH, D_MODEL, PEER_HEADS * D_QUERY)) * D_MODEL ** -0.5
    peer_subkeys = nrm(ks[18], (DEPTH, 2, N_KEYS, D_KEY)) * D_KEY ** -0.5
    peer_u = nrm(ks[19], (DEPTH, N_EXPERTS, D_MODEL)) * D_MODEL ** -0.5
    peer_v = nrm(ks[20], (DEPTH, N_EXPERTS, D_MODEL)) * BETA
    return {'x': x, 'a_w_in': a_w_in, 'a_b_in': a_b_in, 'a_v_g': a_v_g, 'a_v_b': a_v_b,
            'a_w_s': a_w_s, 'a_b_s': a_b_s, 'a_w_out': a_w_out,
            'b_w_in': b_w_in, 'b_w_grp': b_w_grp, 'b_b_grp': b_b_grp, 'b_scale': b_scale, 'b_w_out': b_w_out,
            'ln_mix_g': ln_mix_g, 'ln_mix_b': ln_mix_b, 'ln_ffn_g': ln_ffn_g, 'ln_ffn_b': ln_ffn_b,
            'peer_wq': peer_wq, 'peer_subkeys': peer_subkeys, 'peer_u': peer_u, 'peer_v': peer_v}


def reference(x, a_w_in, a_b_in, a_v_g, a_v_b, a_w_s, a_b_s, a_w_out,
              b_w_in, b_w_grp, b_b_grp, b_scale, b_w_out,
              ln_mix_g, ln_mix_b, ln_ffn_g, ln_ffn_b,
              peer_wq, peer_subkeys, peer_u, peer_v):
    for i in range(DEPTH):
        j = i // N_MIXERS
        if i % N_MIXERS == 0:
            mix = mixer_gmlp_chunked(x, a_w_in[j], a_b_in[j], a_v_g[j], a_v_b[j],
                                     a_w_s[j], a_b_s[j], a_w_out[j])
        else:
            mix = mixer_pool_multiscale(x, b_w_in[j], b_w_grp[j], b_b_grp[j], b_scale[j], b_w_out[j])
        x = layer_norm(ALPHA * x + mix, ln_mix_g[i], ln_mix_b[i])
        ffn = peer_ffn(x, peer_wq[i], peer_subkeys[i], peer_u[i], peer_v[i])
        x = layer_norm(ALPHA * x + ffn, ln_ffn_g[i], ln_ffn_b[i])
    return x
```

```python
import functools

import jax
import jax.numpy as jnp
from jax import lax
from jax.experimental import pallas as pl
from jax.experimental.pallas import tpu as pltpu

LN_EPS = 1e-5
POOL_WINDOWS = (2, 4, 8, 16)
TOPK = 16
POOL_HALO = 16
V7X_VMEM_LIMIT_BYTES = 56 * 1024 * 1024
F32 = jnp.float32
BF16 = jnp.bfloat16


def _params(semantics):
    return pltpu.CompilerParams(dimension_semantics=semantics,
                                vmem_limit_bytes=V7X_VMEM_LIMIT_BYTES)


def _layer_norm_rows(r, g, b):
    mu = jnp.mean(r, axis=-1, keepdims=True)
    rc = r - mu
    var = jnp.mean(rc * rc, axis=-1, keepdims=True)
    return rc * lax.rsqrt(var + LN_EPS) * g + b


def _mm_bias_gelu_kernel(x_ref, w_ref, b_ref, o_ref):
    acc = jnp.dot(x_ref[...], w_ref[...], preferred_element_type=F32)
    o_ref[...] = jax.nn.gelu(acc + b_ref[...]).astype(o_ref.dtype)


def _mm_bias_gelu(x, w, b, *, tm, tn):
    m, k = x.shape
    n = w.shape[1]
    tm, tn = min(tm, m), min(tn, n)
    return pl.pallas_call(
        _mm_bias_gelu_kernel,
        grid=(n // tn, m // tm),
        in_specs=[pl.BlockSpec((tm, k), lambda j, i: (i, 0)),
                  pl.BlockSpec((k, tn), lambda j, i: (0, j)),
                  pl.BlockSpec((1, tn), lambda j, i: (0, j))],
        out_specs=pl.BlockSpec((tm, tn), lambda j, i: (i, j)),
        out_shape=jax.ShapeDtypeStruct((m, n), BF16),
        compiler_params=_params(("arbitrary", "arbitrary")),
        name="mm_bias_gelu",
    )(x, w, b)


def _gmlp_gate_kernel(u_ref, v_ref, g_ref, b_ref, ws_ref, bs_ref, o_ref, *,
                      n_groups, chunk, d_group):
    tm = u_ref.shape[0]
    row = lax.broadcasted_iota(jnp.int32, (chunk, chunk), 0)
    col = lax.broadcasted_iota(jnp.int32, (chunk, chunk), 1)
    causal = col <= row
    for c in range(tm // chunk):
        rows = slice(c * chunk, (c + 1) * chunk)
        vn = _layer_norm_rows(v_ref[rows, :].astype(F32), g_ref[...], b_ref[...]).astype(BF16)
        for h in range(n_groups):
            cols = slice(h * d_group, (h + 1) * d_group)
            w = jnp.where(causal, ws_ref[h], 0.0).astype(BF16)
            sv = jnp.dot(w, vn[:, cols], preferred_element_type=F32) + bs_ref[:, cols]
            o_ref[rows, cols] = (u_ref[rows, cols].astype(F32) * sv).astype(o_ref.dtype)


def _gmlp_gate(z, v_g, v_b, w_s, bs_rows, *, tm):
    m = z.shape[0]
    d_half = z.shape[1] // 2
    n_groups, chunk, _ = w_s.shape
    tm = min(tm, m)
    kern = functools.partial(_gmlp_gate_kernel, n_groups=n_groups, chunk=chunk,
                             d_group=d_half // n_groups)
    return pl.pallas_call(
        kern,
        grid=(m // tm,),
        in_specs=[pl.BlockSpec((tm, d_half), lambda i: (i, 0)),
                  pl.BlockSpec((tm, d_half), lambda i: (i, 1)),
                  pl.BlockSpec((1, d_half), lambda i: (0, 0)),
                  pl.BlockSpec((1, d_half), lambda i: (0, 0)),
                  pl.BlockSpec((n_groups, chunk, chunk), lambda i: (0, 0, 0)),
                  pl.BlockSpec((chunk, d_half), lambda i: (0, 0))],
        out_specs=pl.BlockSpec((tm, d_half), lambda i: (i, 0)),
        out_shape=jax.ShapeDtypeStruct((m, d_half), BF16),
        compiler_params=_params(("arbitrary",)),
        name="gmlp_gate",
    )(z, z, v_g, v_b, w_s, bs_rows)


def _mm_resid_ln_kernel(a_ref, w_ref, x_ref, g_ref, b_ref, y_ref, yt_ref, acc_ref, *, alpha):
    k = pl.program_id(1)

    @pl.when(k == 0)
    def _():
        acc_ref[...] = jnp.zeros_like(acc_ref)

    acc_ref[...] += jnp.dot(a_ref[...], w_ref[...], preferred_element_type=F32)

    @pl.when(k == pl.num_programs(1) - 1)
    def _():
        y = _layer_norm_rows(alpha * x_ref[...] + acc_ref[...], g_ref[...], b_ref[...])
        y_ref[...] = y
        yt_ref[...] = y.T.astype(yt_ref.dtype)


def _mm_resid_ln(a, w, x, g, b, *, alpha, tm, tk):
    m, kdim = a.shape
    n = w.shape[1]
    tm, tk = min(tm, m), min(tk, kdim)
    return pl.pallas_call(
        functools.partial(_mm_resid_ln_kernel, alpha=alpha),
        grid=(m // tm, kdim // tk),
        in_specs=[pl.BlockSpec((tm, tk), lambda i, k: (i, k)),
                  pl.BlockSpec((tk, n), lambda i, k: (k, 0)),
                  pl.BlockSpec((tm, n), lambda i, k: (i, 0)),
                  pl.BlockSpec((1, n), lambda i, k: (0, 0)),
                  pl.BlockSpec((1, n), lambda i, k: (0, 0))],
        out_specs=[pl.BlockSpec((tm, n), lambda i, k: (i, 0)),
                   pl.BlockSpec((n, tm), lambda i, k: (0, i))],
        out_shape=[jax.ShapeDtypeStruct((m, n), F32),
                   jax.ShapeDtypeStruct((n, m), BF16)],
        scratch_shapes=[pltpu.VMEM((tm, n), F32)],
        compiler_params=_params(("arbitrary", "arbitrary")),
        name="mm_resid_ln",
    )(a, w, x, g, b)


def _pool_mix_kernel(xm_ref, xh_ref, win_ref, wg_ref, bg_ref, sc_ref, o_ref,
                     ext_ref, h_ref, *, tiles_per_seq, windows, d_group):
    i = pl.program_id(0)
    tm = xm_ref.shape[0]
    tile_in_seq = i % tiles_per_seq
    ext_ref[0:POOL_HALO, :] = xh_ref[...]
    ext_ref[POOL_HALO:, :] = xm_ref[...]
    h = jnp.dot(ext_ref[...], win_ref[...], preferred_element_type=F32)
    ext_row = lax.broadcasted_iota(jnp.int32, h.shape, 0)
    h_ref[...] = jnp.where((ext_row < POOL_HALO) & (tile_in_seq == 0), 0.0, h)
    pos = tile_in_seq * tm + lax.broadcasted_iota(jnp.int32, (tm, d_group), 0)
    for g, w in enumerate(windows):
        cols = slice(g * d_group, (g + 1) * d_group)
        cur = h_ref[POOL_HALO:, cols]
        total = cur
        for s in range(1, w):
            total = total + h_ref[POOL_HALO - s:POOL_HALO - s + tm, cols]
        count = jnp.minimum(pos + 1, w).astype(F32)
        diff = (total / count - cur).astype(BF16)
        mixed = jnp.dot(diff, wg_ref[g], preferred_element_type=F32) + bg_ref[:, cols]
        o_ref[:, cols] = (mixed * sc_ref[:, cols]).astype(o_ref.dtype)


def _pool_mix(xb, w_in, w_grp, b_grp, scale, *, seq_len, tm):
    m, d = xb.shape
    n_groups, d_group, _ = w_grp.shape
    d_mix = n_groups * d_group
    tm = min(tm, seq_len)
    kern = functools.partial(_pool_mix_kernel, tiles_per_seq=seq_len // tm,
                             windows=POOL_WINDOWS, d_group=d_group)
    halo_blocks = tm // POOL_HALO
    return pl.pallas_call(
        kern,
        grid=(m // tm,),
        in_specs=[pl.BlockSpec((tm, d), lambda i: (i, 0)),
                  pl.BlockSpec((POOL_HALO, d), lambda i: (jnp.maximum(i * halo_blocks - 1, 0), 0)),
                  pl.BlockSpec((d, d_mix), lambda i: (0, 0)),
                  pl.BlockSpec((n_groups, d_group, d_group), lambda i: (0, 0, 0)),
                  pl.BlockSpec((1, d_mix), lambda i: (0, 0)),
                  pl.BlockSpec((1, d_mix), lambda i: (0, 0))],
        out_specs=pl.BlockSpec((tm, d_mix), lambda i: (i, 0)),
        out_shape=jax.ShapeDtypeStruct((m, d_mix), BF16),
        scratch_shapes=[pltpu.VMEM((tm + POOL_HALO, d), BF16),
                        pltpu.VMEM((tm + POOL_HALO, d_mix), F32)],
        compiler_params=_params(("arbitrary",)),
        name="pool_mix",
    )(xb, xb, w_in, w_grp, b_grp, scale)


def _top_ranks(s, vals_ref, half, n_keys):
    iota = lax.broadcasted_iota(jnp.int32, s.shape, 0)
    rank = jnp.full(s.shape, float(n_keys), F32)
    cur = s
    for a in range(TOPK):
        m = jnp.max(cur, axis=0, keepdims=True)
        idx = jnp.min(jnp.where(cur == m, iota, n_keys), axis=0, keepdims=True)
        hit = iota == idx
        rank = jnp.where(hit, float(a), rank)
        cur = jnp.where(hit, -jnp.inf, cur)
        vals_ref[half, a:a + 1, :] = m
    return rank


def _router_kernel(xt_ref, wqt_ref, keys_ref, nn_ref, f1_ref, r2_ref, e2_ref,
                   q_ref, vals_ref, *, n_keys):
    h = pl.program_id(1)
    tm = xt_ref.shape[1]
    d_key = keys_ref.shape[2]

    @pl.when(h == 0)
    def _():
        q_ref[...] = jnp.dot(wqt_ref[...], xt_ref[...], preferred_element_type=F32)

    base = pl.multiple_of(h * (2 * d_key), 2 * d_key)
    q1 = q_ref[pl.ds(base, d_key), :].astype(BF16)
    q2 = q_ref[pl.ds(base + d_key, d_key), :].astype(BF16)
    s1 = jnp.dot(keys_ref[0], q1, preferred_element_type=F32)
    s2 = jnp.dot(keys_ref[1], q2, preferred_element_type=F32)
    r1 = _top_ranks(s1, vals_ref, 0, n_keys)
    r2 = _top_ranks(s2, vals_ref, 1, n_keys)
    v1 = vals_ref[0]
    v2 = vals_ref[1]

    half = TOPK // 2
    i16 = lax.broadcasted_iota(jnp.int32, (TOPK, tm), 0)
    i8 = lax.broadcasted_iota(jnp.int32, (half, tm), 0)
    blocks = [v1 + v2[0:1]]
    flat = [i16 * TOPK]
    for b in range(1, half):
        blocks.append(v1[0:half] + v2[b:b + 1])
        flat.append(i8 * TOPK + b)
    blocks.append(v1[0:1] + v2[half:TOPK])
    flat.append(i8 + half)
    cand = jnp.concatenate(blocks, axis=0)
    fidx = jnp.concatenate(flat, axis=0)
    cmax = cand[0:1]

    sel = jnp.zeros(cand.shape, F32)
    cur = cand
    for _ in range(TOPK):
        m = jnp.max(cur, axis=0, keepdims=True)
        idx = jnp.min(jnp.where(cur == m, fidx, TOPK * TOPK), axis=0, keepdims=True)
        hit = fidx == idx
        sel = jnp.where(hit, 1.0, sel)
        cur = jnp.where(hit, -jnp.inf, cur)

    denom = jnp.sum(sel * jnp.exp(cand - cmax), axis=0, keepdims=True)
    n_lo = sel[0:half]
    for b in range(1, half):
        n_lo = n_lo + sel[TOPK + (b - 1) * half:TOPK + b * half]
    n_row0 = jnp.sum(sel[TOPK + (half - 1) * half:], axis=0, keepdims=True)
    n_lo = n_lo + jnp.where(i8 == 0, n_row0, 0.0)
    n_by_rank = jnp.concatenate([n_lo, sel[half:TOPK]], axis=0)

    nn = jnp.zeros(s1.shape, F32)
    for a in range(TOPK):
        nn = jnp.where(r1 == float(a), n_by_rank[a:a + 1], nn)

    nn_ref[0] = nn
    f1_ref[0] = jnp.exp(s1 - v1[0:1]) / denom
    r2_ref[0] = r2
    e2_ref[0] = jnp.exp(s2 - v2[0:1])


def _router(xt, wqt, keys, *, tm):
    d, t = xt.shape
    _, n_keys, d_key = keys.shape
    n_heads = wqt.shape[0] // (2 * d_key)
    tm = min(tm, t)
    tab = jax.ShapeDtypeStruct((n_heads, n_keys, t), F32)
    tab_spec = pl.BlockSpec((1, n_keys, tm), lambda i, h: (h, 0, i))
    return pl.pallas_call(
        functools.partial(_router_kernel, n_keys=n_keys),
        grid=(t // tm, n_heads),
        in_specs=[pl.BlockSpec((d, tm), lambda i, h: (0, i)),
                  pl.BlockSpec(wqt.shape, lambda i, h: (0, 0)),
                  pl.BlockSpec(keys.shape, lambda i, h: (0, 0, 0))],
        out_specs=[tab_spec] * 4,
        out_shape=[tab] * 4,
        scratch_shapes=[pltpu.VMEM((wqt.shape[0], tm), F32),
                        pltpu.VMEM((2, TOPK, tm), F32)],
        compiler_params=_params(("arbitrary", "arbitrary")),
        name="peer_router",
    )(xt, wqt, keys)


def _expert_kernel(xt_ref, u_ref, vt_ref, nn_ref, f1_ref, r2_ref, e2_ref,
                   x_ref, g_ref, b_ref, o_ref, ob_ref, acc_ref, act_ref, *,
                   alpha, n_keys, lane_blk):
    e = pl.program_id(1)
    te = u_ref.shape[0]
    tm = xt_ref.shape[1]
    n_heads = nn_ref.shape[0]
    rows_per_step = te // n_keys

    @pl.when(e == 0)
    def _():
        acc_ref[...] = jnp.zeros_like(acc_ref)

    for lb in range(tm // lane_blk):
        lanes = slice(lb * lane_blk, (lb + 1) * lane_blk)
        pre = jnp.dot(u_ref[...], xt_ref[:, lanes], preferred_element_type=F32)
        for ci in range(rows_per_step):
            c = e * rows_per_step + ci
            gate = jnp.zeros((n_keys, lane_blk), F32)
            for h in range(n_heads):
                n_sel = nn_ref[h, pl.ds(c, 1), lanes]
                f1 = f1_ref[h, pl.ds(c, 1), lanes]
                gate = gate + jnp.where(r2_ref[h, :, lanes] < n_sel, f1 * e2_ref[h, :, lanes], 0.0)
            rows = slice(ci * n_keys, (ci + 1) * n_keys)
            act_ref[rows, lanes] = (gate * jax.nn.gelu(pre[rows])).astype(act_ref.dtype)
        acc_ref[:, lanes] += jnp.dot(vt_ref[...], act_ref[:, lanes], preferred_element_type=F32)

    @pl.when(e == pl.num_programs(1) - 1)
    def _():
        y = _layer_norm_rows(alpha * x_ref[...] + acc_ref[...].T, g_ref[...], b_ref[...])
        o_ref[...] = y
        ob_ref[...] = y.astype(ob_ref.dtype)


def _experts(xt, u, vt, tables, x, g, b, *, alpha, n_keys, tm, te, lane_blk):
    d, t = xt.shape
    n_exp = u.shape[0]
    n_heads = tables[0].shape[0]
    tm, te = min(tm, t), min(te, n_exp)
    lane_blk = min(lane_blk, tm)
    tab_spec = pl.BlockSpec((n_heads, n_keys, tm), lambda i, e: (0, 0, i),
                            pipeline_mode=pl.Buffered(1))
    kern = functools.partial(_expert_kernel, alpha=alpha, n_keys=n_keys, lane_blk=lane_blk)
    return pl.pallas_call(
        kern,
        grid=(t // tm, n_exp // te),
        in_specs=[pl.BlockSpec((d, tm), lambda i, e: (0, i)),
                  pl.BlockSpec((te, d), lambda i, e: (e, 0)),
                  pl.BlockSpec((d, te), lambda i, e: (0, e)),
                  tab_spec, tab_spec, tab_spec, tab_spec,
                  pl.BlockSpec((tm, d), lambda i, e: (i, 0)),
                  pl.BlockSpec((1, d), lambda i, e: (0, 0)),
                  pl.BlockSpec((1, d), lambda i, e: (0, 0))],
        out_specs=[pl.BlockSpec((tm, d), lambda i, e: (i, 0)),
                   pl.BlockSpec((tm, d), lambda i, e: (i, 0))],
        out_shape=[jax.ShapeDtypeStruct((t, d), F32),
                   jax.ShapeDtypeStruct((t, d), BF16)],
        scratch_shapes=[pltpu.VMEM((d, tm), F32),
                        pltpu.VMEM((te, tm), BF16)],
        compiler_params=_params(("arbitrary", "arbitrary")),
        name="peer_experts",
    )(xt, u, vt, *tables, x, g, b)


def _peer_layer(x, xt, wq, subkeys, u_tab, v_tab, g, b, *, alpha):
    n_keys = subkeys.shape[1]
    tables = _router(xt, wq.T.astype(BF16), subkeys.astype(BF16), tm=512)
    return _experts(xt, u_tab.astype(BF16), v_tab.T.astype(BF16), tables, x, g, b,
                    alpha=alpha, n_keys=n_keys, tm=512, te=512, lane_blk=256)


def kernel(x, a_w_in, a_b_in, a_v_g, a_v_b, a_w_s, a_b_s, a_w_out, b_w_in, b_w_grp, b_b_grp, b_scale, b_w_out, ln_mix_g, ln_mix_b, ln_ffn_g, ln_ffn_b, peer_wq, peer_subkeys, peer_u, peer_v):
    bsz, seq, d = x.shape
    depth = ln_mix_g.shape[0]
    alpha = (2 * depth) ** 0.25
    xf = x.reshape(bsz * seq, d)
    xb = xf.astype(BF16)
    row = lambda v: v.reshape(1, -1)
    for i in range(depth):
        j = i // 2
        if i % 2 == 0:
            d_half = a_w_out.shape[1]
            d_group = d_half // a_w_s.shape[1]
            z = _mm_bias_gelu(xb, a_w_in[j].astype(BF16), row(a_b_in[j]), tm=1024, tn=1024)
            bs_rows = jnp.repeat(a_b_s[j].T, d_group, axis=1)
            mixed = _gmlp_gate(z, row(a_v_g[j]), row(a_v_b[j]), a_w_s[j], bs_rows, tm=256)
            w_out = a_w_out[j]
        else:
            mixed = _pool_mix(xb, b_w_in[j].astype(BF16), b_w_grp[j].astype(BF16),
                              row(b_b_grp[j]), row(b_scale[j]), seq_len=seq, tm=256)
            w_out = b_w_out[j]
        xf, xt = _mm_resid_ln(mixed, w_out.astype(BF16), xf, row(ln_mix_g[i]), row(ln_mix_b[i]),
                              alpha=alpha, tm=512, tk=1024)
        xf, xb = _peer_layer(xf, xt, peer_wq[i], peer_subkeys[i], peer_u[i], peer_v[i],
                             row(ln_ffn_g[i]), row(ln_ffn_b[i]), alpha=alpha)
    return xf.reshape(bsz, seq, d)
```
